```python
import jax, jax.numpy as jnp
from jax import lax

D_MODEL = 2048
BATCH = 2
SEQ = 4096
DEPTH = 4
DEC_BATCH = 128
DEC_SEQ = 1
PAST_LEN = 8192
PAGE_SIZE = 128

H_A = 8
HD_A = 64
CMP_LEN = 32
CMP_STRIDE = 16
CMP_HID = 256
SEL_BLOCK = 64
SEL_TOPN = 16
WINDOW = 512
H_B = 8
R_Q = 384
R_KV = 128
D_NOPE = 64
D_ROPE = 32
DV_B = 64
ROPE_THETA = 10000.0
H_C = 8
HD_C = 64
MOBA_BLOCK = 256
MOBA_TOPK = 3
N_MEM = 256
H_D = 4
HD_D = 128
N_BRANCH = 4
BR_WIDTH = 512
PEER_HEADS = 8
PEER_DK = 256
N_KEYS = 128
N_EXP = N_KEYS * N_KEYS
PEER_TOPK = 16
ALPHA = (2 * DEPTH) ** 0.25
BETA = (8 * DEPTH) ** -0.25
Q_BLOCK = 128
MOBA_Q_BLOCK = 32
TOK_BLOCK = 128
LN_EPS = 1e-5
RMS_EPS = 1e-6
NEG = -1e30
FORCE_SCORE = 1e9
IN_SIZES = (H_A * HD_A, 6 * HD_A, 3 * H_A, R_Q, R_KV, D_ROPE, H_C * HD_C, 2 * HD_C, H_D * HD_D, N_BRANCH * D_MODEL)
N_IN = sum(IN_SIZES)

kernel_name = 'nsa_mla_moba_peer_hybrid_step'


def layer_norm(x, g, b):
    xf = x.astype(jnp.float32)
    mu = jnp.mean(xf, -1, keepdims=True)
    var = jnp.mean(jnp.square(xf - mu), -1, keepdims=True)
    return ((xf - mu) * lax.rsqrt(var + LN_EPS) * g + b).astype(x.dtype)


def rms_norm(x, g):
    xf = x.astype(jnp.float32)
    return (xf * lax.rsqrt(jnp.mean(xf * xf, -1, keepdims=True) + RMS_EPS) * g).astype(x.dtype)


def rope(x, pos):
    half = D_ROPE // 2
    freqs = jnp.power(ROPE_THETA, -jnp.arange(half, dtype=jnp.float32) / half)
    ang = pos.astype(jnp.float32)[:, None] * freqs[None, :]
    ang = ang.reshape((1, pos.shape[0]) + (1,) * (x.ndim - 3) + (half,))
    cos, sin = jnp.cos(ang), jnp.sin(ang)
    x1 = x[..., :half].astype(jnp.float32)
    x2 = x[..., half:].astype(jnp.float32)
    return jnp.concatenate([x1 * cos - x2 * sin, x1 * sin + x2 * cos], -1).astype(x.dtype)


def alibi_slopes(n):
    return jnp.power(2.0, -8.0 * jnp.arange(1, n + 1, dtype=jnp.float32) / n)


def masked_softmax(s, mask):
    s = jnp.where(mask, s, NEG)
    m = jnp.max(s, -1, keepdims=True)
    p = jnp.exp(s - m) * mask
    return p / jnp.maximum(jnp.sum(p, -1, keepdims=True), 1e-30)


def split_cols(y, sizes):
    offs, acc = [], 0
    for s in sizes[:-1]:
        acc += s
        offs.append(acc)
    return jnp.split(y, offs, axis=-1)


def sweep_queries(fn, q_pos, qs, block):
    n = q_pos.shape[0]
    qb = block if n % block == 0 else n
    nc = n // qb
    pos_c = q_pos.reshape(nc, qb)
    qs_c = tuple(jnp.moveaxis(q.reshape((q.shape[0], nc, qb) + q.shape[2:]), 1, 0) for q in qs)
    out = lax.map(lambda a: fn(a[0], *a[1:]), (pos_c,) + qs_c)
    out = jnp.moveaxis(out, 0, 1)
    return out.reshape((out.shape[0], n) + out.shape[3:])


def gather_pages(pool, page_table):
    g = pool[page_table]
    return g.reshape((page_table.shape[0], page_table.shape[1] * pool.shape[1]) + pool.shape[2:])


def nsa_attention(q, gate, q_pos, k_cmp, v_cmp, k_slc, v_slc, k_win, v_win, win_start,
                  cmp_pos, cmp_w1, cmp_b1, cmp_w2):
    B, L, _ = k_cmp.shape
    scale = HD_A ** -0.5
    slopes = alibi_slopes(H_A)
    n_cmp = (L - CMP_LEN) // CMP_STRIDE + 1
    cmp_idx = CMP_STRIDE * jnp.arange(n_cmp)[:, None] + jnp.arange(CMP_LEN)[None, :]

    def compress(rows, i):
        blk = rows[:, cmp_idx] + cmp_pos[i]
        hid = jax.nn.gelu(blk.reshape(B, n_cmp, CMP_LEN * HD_A) @ cmp_w1[i] + cmp_b1[i], approximate=False)
        return hid @ cmp_w2[i]

    kc, vc = compress(k_cmp, 0), compress(v_cmp, 1)
    cmp_end = cmp_idx[:, -1]
    n_sel = -(-L // SEL_BLOCK)
    pad = n_sel * SEL_BLOCK - L
    ks_b = jnp.pad(k_slc, ((0, 0), (0, pad), (0, 0))).reshape(B, n_sel, SEL_BLOCK, HD_A)
    vs_b = jnp.pad(v_slc, ((0, 0), (0, pad), (0, 0))).reshape(B, n_sel, SEL_BLOCK, HD_A)
    c0 = CMP_STRIDE * jnp.arange(n_cmp)
    s0 = SEL_BLOCK * jnp.arange(n_sel)
    overlap = jnp.clip(jnp.minimum(c0[:, None] + CMP_LEN, s0[None, :] + SEL_BLOCK)
                       - jnp.maximum(c0[:, None], s0[None, :]), 0).astype(jnp.float32) / CMP_LEN
    k_top = min(SEL_TOPN, n_sel)
    blk_ids = jnp.arange(n_sel)
    kw_pad = jnp.pad(k_win, ((0, 0), (WINDOW, 0), (0, 0)))
    vw_pad = jnp.pad(v_win, ((0, 0), (WINDOW, 0), (0, 0)))

    def chunk(tp, qc, gc):
        qb = tp.shape[0]
        dist = (tp[:, None] - cmp_end[None, :]).astype(jnp.float32)
        s = jnp.einsum('bqhd,bcd->bqhc', qc, kc).astype(jnp.float32) * scale - slopes[:, None] * dist[:, None, :]
        p_cmp = masked_softmax(s, (cmp_end[None, :] <= tp[:, None])[:, None, :])
        o_cmp = jnp.einsum('bqhc,bcd->bqhd', p_cmp.astype(vc.dtype), vc)
        imp = jnp.einsum('bqhc,cj->bqj', p_cmp, overlap)
        cur = tp // SEL_BLOCK
        valid = blk_ids[None, :] * SEL_BLOCK <= tp[:, None]
        forced = (blk_ids[None, :] == 0) | (blk_ids[None, :] == cur[:, None]) | (blk_ids[None, :] == cur[:, None] - 1)
        imp = jnp.where(valid, jnp.where(forced, FORCE_SCORE, imp), NEG)
        _, sel = lax.top_k(imp, k_top)
        kg = jax.vmap(lambda kb, s_: kb[s_])(ks_b, sel).reshape(B, qb, k_top * SEL_BLOCK, HD_A)
        vg = jax.vmap(lambda vb, s_: vb[s_])(vs_b, sel).reshape(B, qb, k_top * SEL_BLOCK, HD_A)
        kp = (sel[..., None] * SEL_BLOCK + jnp.arange(SEL_BLOCK)).reshape(B, qb, k_top * SEL_BLOCK)
        ok = kp <= tp[None, :, None]
        s = (jnp.einsum('bqhd,bqkd->bqhk', qc, kg).astype(jnp.float32) * scale
             - slopes[None, None, :, None] * (tp[None, :, None, None] - kp[:, :, None, :]).astype(jnp.float32))
        o_slc = jnp.einsum('bqhk,bqkd->bqhd', masked_softmax(s, ok[:, :, None, :]).astype(vg.dtype), vg)
        wb = WINDOW + qb - 1
        start = tp[0] + 1 - win_start
        kw = lax.dynamic_slice_in_dim(kw_pad, start, wb, axis=1)
        vw = lax.dynamic_slice_in_dim(vw_pad, start, wb, axis=1)
        bp = tp[0] - WINDOW + 1 + jnp.arange(wb)
        dw = tp[:, None] - bp[None, :]
        okw = (bp[None, :] >= win_start) & (dw >= 0) & (dw < WINDOW)
        s = jnp.einsum('bqhd,bkd->bqhk', qc, kw).astype(jnp.float32) * scale - slopes[:, None] * dw.astype(jnp.float32)[:, None, :]
        o_win = jnp.einsum('bqhk,bkd->bqhd', masked_softmax(s, okw[:, None, :]).astype(vw.dtype), vw)
        return gc[..., 0:1] * o_cmp + gc[..., 1:2] * o_slc + gc[..., 2:3] * o_win

    return sweep_queries(chunk, q_pos, (q, gate), Q_BLOCK)


def mla_attention(q_nope, q_rope, q_pos, ckv, krope, w_uk, w_uv):
    L = ckv.shape[1]
    kpos = jnp.arange(L)
    scale = (D_NOPE + D_ROPE) ** -0.5
    q_lat = jnp.einsum('bqhn,rhn->bqhr', q_nope, w_uk)

    def chunk(tp, ql, qr):
        s = (jnp.einsum('bqhr,bkr->bqhk', ql, ckv) + jnp.einsum('bqhe,bke->bqhk', qr, krope)).astype(jnp.float32) * scale
        p = masked_softmax(s, (kpos[None, :] <= tp[:, None])[:, None, :])
        return jnp.einsum('bqhk,bkr->bqhr', p.astype(ckv.dtype), ckv)

    o_lat = sweep_queries(chunk, q_pos, (q_lat, q_rope), Q_BLOCK)
    return jnp.einsum('bqhr,rhv->bqhv', o_lat, w_uv)


def moba_attention(q, q_pos, k, v):
    B, L, _ = k.shape
    scale = HD_C ** -0.5
    slopes = alibi_slopes(H_C)
    nb = -(-L // MOBA_BLOCK)
    pad = nb * MOBA_BLOCK - L
    kb = jnp.pad(k, ((0, 0), (0, pad), (0, 0))).reshape(B, nb, MOBA_BLOCK, HD_C)
    vb = jnp.pad(v, ((0, 0), (0, pad), (0, 0))).reshape(B, nb, MOBA_BLOCK, HD_C)
    kmean = jnp.mean(kb.astype(jnp.float32), axis=2).astype(k.dtype)
    k_top = min(MOBA_TOPK, nb)
    blk_ids = jnp.arange(nb)
    in_blk = jnp.arange(MOBA_BLOCK)
    n_sk = k_top * MOBA_BLOCK

    def chunk(tp, qc):
        qb = tp.shape[0]
        cur = tp // MOBA_BLOCK
        gs = jnp.einsum('bqhd,bnd->bqhn', qc, kmean).astype(jnp.float32)
        past = blk_ids[None, :] < cur[:, None]
        gs = jnp.where(past[:, None, :], gs, NEG)
        _, sel = lax.top_k(gs, k_top)
        sel_ok = sel < cur[None, :, None, None]
        kg = jax.vmap(lambda kb_, s_: kb_[s_])(kb, sel)
        vg = jax.vmap(lambda vb_, s_: vb_[s_])(vb, sel)
        kp = sel[..., None] * MOBA_BLOCK + in_blk
        s_sel = (jnp.einsum('bqhd,bqhkmd->bqhkm', qc, kg).astype(jnp.float32) * scale
                 - slopes[None, None, :, None, None] * (tp[None, :, None, None, None] - kp).astype(jnp.float32))
        ok_sel = jnp.broadcast_to(sel_ok[..., None], kp.shape)
        ko = kb[:, cur]
        vo = vb[:, cur]
        op = cur[:, None] * MOBA_BLOCK + in_blk[None, :]
        s_own = (jnp.einsum('bqhd,bqmd->bqhm', qc, ko).astype(jnp.float32) * scale
                 - slopes[:, None] * (tp[:, None] - op).astype(jnp.float32)[:, None, :])
        ok_own = jnp.broadcast_to((op <= tp[:, None])[None, :, None, :], s_own.shape)
        s_all = jnp.concatenate([s_sel.reshape(B, qb, H_C, n_sk), s_own], -1)
        ok_all = jnp.concatenate([ok_sel.reshape(B, qb, H_C, n_sk), ok_own], -1)
        p = masked_softmax(s_all, ok_all).astype(v.dtype)
        return (jnp.einsum('bqhkm,bqhkmd->bqhd', p[..., :n_sk].reshape(B, qb, H_C, k_top, MOBA_BLOCK), vg)
                + jnp.einsum('bqhm,bqmd->bqhd', p[..., n_sk:], vo))

    return sweep_queries(chunk, q_pos, (q,), MOBA_Q_BLOCK)


def memory_attention(q, mk, mv):
    s = jnp.einsum('bqhd,bmhd->bqhm', q, mk).astype(jnp.float32) * HD_D ** -0.5
    p = jax.nn.softmax(s, axis=-1).astype(mv.dtype)
    return jnp.einsum('bqhm,bmhd->bqhd', p, mv)


def peer_ffn(x, wq, sub_keys, u, v):
    B, T, D = x.shape
    n = B * T
    nc = -(-n // TOK_BLOCK)
    xt = jnp.pad(x.reshape(n, D), ((0, nc * TOK_BLOCK - n), (0, 0))).reshape(nc, TOK_BLOCK, D)
    half = PEER_DK // 2
    n_cand = PEER_TOPK * PEER_TOPK

    def chunk(xc):
        q = (xc @ wq).reshape(TOK_BLOCK, PEER_HEADS, 2, half)
        s1 = jnp.einsum('thc,kc->thk', q[:, :, 0], sub_keys[0]).astype(jnp.float32)
        s2 = jnp.einsum('thc,kc->thk', q[:, :, 1], sub_keys[1]).astype(jnp.float32)
        v1, i1 = lax.top_k(s1, PEER_TOPK)
        v2, i2 = lax.top_k(s2, PEER_TOPK)
        cand = (v1[..., :, None] + v2[..., None, :]).reshape(TOK_BLOCK, PEER_HEADS, n_cand)
        cidx = (i1[..., :, None] * N_KEYS + i2[..., None, :]).reshape(TOK_BLOCK, PEER_HEADS, n_cand)
        sv, si = lax.top_k(cand, PEER_TOPK)
        eidx = jnp.take_along_axis(cidx, si, axis=-1)
        g = jax.nn.softmax(sv, axis=-1)
        act = jax.nn.gelu(jnp.einsum('td,thkd->thk', xc, u[eidx]).astype(jnp.float32), approximate=False)
        return jnp.einsum('thk,thkd->td', (g * act).astype(xc.dtype), v[eidx])

    out = lax.map(chunk, xt)
    return out.reshape(nc * TOK_BLOCK, D)[:n].reshape(B, T, D)


def trunk_layer(x, start, past, mem_k, mem_v, p):
    B, T, _ = x.shape
    pos = start + jnp.arange(T, dtype=jnp.int32)
    a_q, a_kv, a_g, b_cq, b_ckv, b_kr, c_q, c_kv, d_q, g_pre = split_cols(x @ p['w_in'], IN_SIZES)
    a_q = a_q.reshape(B, T, H_A, HD_A)
    a_g = jax.nn.sigmoid(a_g.reshape(B, T, H_A, 3))
    k_cmp, v_cmp, k_slc, v_slc, k_win, v_win = jnp.split(a_kv, 6, axis=-1)
    q_b = jnp.einsum('btr,rhe->bthe', rms_norm(b_cq, p['q_norm']), p['w_uq'])
    q_nope, q_rope = q_b[..., :D_NOPE], rope(q_b[..., D_NOPE:], pos)
    ckv = rms_norm(b_ckv, p['kv_norm'])
    krope = rope(b_kr, pos)
    c_q = c_q.reshape(B, T, H_C, HD_C)
    k_mb, v_mb = jnp.split(c_kv, 2, axis=-1)
    d_q = d_q.reshape(B, T, H_D, HD_D)
    gates = jax.nn.sigmoid(g_pre.reshape(B, T, N_BRANCH, D_MODEL))
    rows = (k_cmp, v_cmp, k_slc, v_slc, ckv, krope, k_mb, v_mb)
    if past is None:
        full, win_k, win_v, buf_len = rows, k_win, v_win, min(WINDOW, T)
    else:
        full = tuple(jnp.concatenate([old, new], axis=1) for old, new in zip(past[:8], rows))
        win_k = jnp.concatenate([past[8], k_win], axis=1)
        win_v = jnp.concatenate([past[9], v_win], axis=1)
        buf_len = past[8].shape[1]
    win_start = start + T - win_k.shape[1]
    f_cmp_k, f_cmp_v, f_slc_k, f_slc_v, f_ckv, f_krope, f_mb_k, f_mb_v = full
    o_a = nsa_attention(a_q, a_g, pos, f_cmp_k, f_cmp_v, f_slc_k, f_slc_v, win_k, win_v, win_start,
                        p['cmp_pos'], p['cmp_w1'], p['cmp_b1'], p['cmp_w2'])
    o_b = mla_attention(q_nope, q_rope, pos, f_ckv, f_krope, p['w_uk'], p['w_uv'])
    o_c = moba_attention(c_q, pos, f_mb_k, f_mb_v)
    o_d = memory_attention(d_q, mem_k, mem_v)
    o = jnp.stack([o_a.reshape(B, T, BR_WIDTH), o_b.reshape(B, T, BR_WIDTH),
                   o_c.reshape(B, T, BR_WIDTH), o_d.reshape(B, T, BR_WIDTH)], axis=2)
    z = jnp.einsum('btnw,nwd->btnd', o, p['w_branch'])
    mixed = jnp.einsum('btnd,btnd->btd', gates, z) @ p['w_out']
    h = layer_norm(ALPHA * x + mixed, p['ln1_g'], p['ln1_b'])
    f = peer_ffn(h, p['peer_wq'], p['peer_keys'], p['peer_u'], p['peer_v'])
    y = layer_norm(ALPHA * h + f, p['ln2_g'], p['ln2_b'])
    return y, rows + (win_k[:, -buf_len:], win_v[:, -buf_len:])


def setup_inputs(seed: int = 0) -> dict:
    key = jax.random.key(seed)
    keys = list(jax.random.split(key, 48))

    def rnd(shape, scale):
        return jax.random.normal(keys.pop(), shape, jnp.float32) * scale

    n_pages = PAST_LEN // PAGE_SIZE
    n_used = DEC_BATCH * n_pages
    n_phys = n_used + max(1, n_used // 4)
    win_buf = min(WINDOW, PAST_LEN)
    page_table = jax.random.permutation(keys.pop(), n_phys)[:n_used].reshape(DEC_BATCH, n_pages).astype(jnp.int32)

    def pool(width):
        return rnd((DEPTH, n_phys, PAGE_SIZE, width), 1.0)

    return {
        'x_prompt': rnd((BATCH, SEQ, D_MODEL), 1.0),
        'x_sample': rnd((DEC_BATCH, DEC_SEQ, D_MODEL), 1.0),
        'cache_cmp_k': pool(HD_A),
        'cache_cmp_v': pool(HD_A),
        'cache_slc_k': pool(HD_A),
        'cache_slc_v': pool(HD_A),
        'cache_mla_ckv': pool(R_KV),
        'cache_mla_krope': pool(D_ROPE),
        'cache_moba_k': pool(HD_C),
        'cache_moba_v': pool(HD_C),
        'state_win_k': rnd((DEPTH, DEC_BATCH, win_buf, HD_A), 1.0),
        'state_win_v': rnd((DEPTH, DEC_BATCH, win_buf, HD_A), 1.0),
        'cache_mem_k': rnd((DEPTH, DEC_BATCH, N_MEM, H_D, HD_D), 1.0),
        'cache_mem_v': rnd((DEPTH, DEC_BATCH, N_MEM, H_D, HD_D), 1.0),
        'page_table': page_table,
        'mem_prompt': rnd((BATCH, N_MEM, D_MODEL), 1.0),
        'w_in': rnd((DEPTH, D_MODEL, N_IN), D_MODEL ** -0.5),
        'nsa_cmp_pos': rnd((DEPTH, 2, CMP_LEN, HD_A), 0.1),
        'nsa_cmp_w1': rnd((DEPTH, 2, CMP_LEN * HD_A, CMP_HID), (CMP_LEN * HD_A) ** -0.5),
        'nsa_cmp_b1': rnd((DEPTH, 2, CMP_HID), 0.01),
        'nsa_cmp_w2': rnd((DEPTH, 2, CMP_HID, HD_A), CMP_HID ** -0.5),
        'mla_q_norm': 1.0 + rnd((DEPTH, R_Q), 0.1),
        'mla_kv_norm': 1.0 + rnd((DEPTH, R_KV), 0.1),
        'mla_w_uq': rnd((DEPTH, R_Q, H_B, D_NOPE + D_ROPE), R_Q ** -0.5),
        'mla_w_uk': rnd((DEPTH, R_KV, H_B, D_NOPE), R_KV ** -0.5),
        'mla_w_uv': rnd((DEPTH, R_KV, H_B, DV_B), R_KV ** -0.5),
        'w_mem_kv': rnd((DEPTH, D_MODEL, 2 * H_D * HD_D), D_MODEL ** -0.5),
        'w_branch': rnd((DEPTH, N_BRANCH, BR_WIDTH, D_MODEL), BR_WIDTH ** -0.5),
        'w_out': rnd((DEPTH, D_MODEL, D_MODEL), BETA * D_MODEL ** -0.5),
        'ln1_g': 1.0 + rnd((DEPTH, D_MODEL), 0.1),
        'ln1_b': rnd((DEPTH, D_MODEL), 0.01),
        'ln2_g': 1.0 + rnd((DEPTH, D_MODEL), 0.1),
        'ln2_b': rnd((DEPTH, D_MODEL), 0.01),
        'peer_wq': rnd((DEPTH, D_MODEL, PEER_HEADS * PEER_DK), D_MODEL ** -0.5),
        'peer_keys': rnd((DEPTH, 2, N_KEYS, PEER_DK // 2), (PEER_DK // 2) ** -0.5),
        'peer_u': rnd((DEPTH, N_EXP, D_MODEL), D_MODEL ** -0.5),
        'peer_v': rnd((DEPTH, N_EXP, D_MODEL), BETA * PEER_HEADS ** -0.5),
    }


def reference(x_prompt, x_sample, cache_cmp_k, cache_cmp_v, cache_slc_k, cache_slc_v, cache_mla_ckv,
              cache_mla_krope, cache_moba_k, cache_moba_v, state_win_k, state_win_v, cache_mem_k, cache_mem_v,
              page_table, mem_prompt, w_in, nsa_cmp_pos, nsa_cmp_w1, nsa_cmp_b1, nsa_cmp_w2, mla_q_norm,
              mla_kv_norm, mla_w_uq, mla_w_uk, mla_w_uv, w_mem_kv, w_branch, w_out, ln1_g, ln1_b, ln2_g, ln2_b,
              peer_wq, peer_keys, peer_u, peer_v):
    past_len = page_table.shape[1] * cache_cmp_k.shape[2]
    pools = (cache_cmp_k, cache_cmp_v, cache_slc_k, cache_slc_v, cache_mla_ckv, cache_mla_krope,
             cache_moba_k, cache_moba_v)
    y_prompt, y_sample = x_prompt, x_sample
    rows_p, rows_s = [], []
    for l in range(DEPTH):
        p = {
            'w_in': w_in[l], 'cmp_pos': nsa_cmp_pos[l], 'cmp_w1': nsa_cmp_w1[l], 'cmp_b1': nsa_cmp_b1[l],
            'cmp_w2': nsa_cmp_w2[l], 'q_norm': mla_q_norm[l], 'kv_norm': mla_kv_norm[l], 'w_uq': mla_w_uq[l],
            'w_uk': mla_w_uk[l], 'w_uv': mla_w_uv[l], 'w_branch': w_branch[l], 'w_out': w_out[l],
            'ln1_g': ln1_g[l], 'ln1_b': ln1_b[l], 'ln2_g': ln2_g[l], 'ln2_b': ln2_b[l],
            'peer_wq': peer_wq[l], 'peer_keys': peer_keys[l], 'peer_u': peer_u[l], 'peer_v': peer_v[l],
        }
        mem_kv = jnp.einsum('bmd,de->bme', mem_prompt, w_mem_kv[l]).reshape(mem_prompt.shape[0], N_MEM, 2, H_D, HD_D)
        mem_k, mem_v = mem_kv[:, :, 0], mem_kv[:, :, 1]
        y_prompt, new_p = trunk_layer(y_prompt, 0, None, mem_k, mem_v, p)
        past = tuple(gather_pages(c[l], page_table) for c in pools) + (state_win_k[l], state_win_v[l])
        y_sample, new_s = trunk_layer(y_sample, past_len, past, cache_mem_k[l], cache_mem_v[l], p)
        rows_p.append(new_p + (mem_k, mem_v))
        rows_s.append(new_s)
    (p_cmp_k, p_cmp_v, p_slc_k, p_slc_v, p_mla_ckv, p_mla_krope, p_moba_k, p_moba_v,
     p_win_k, p_win_v, p_mem_k, p_mem_v) = [jnp.stack(r, axis=0) for r in zip(*rows_p)]
    (s_cmp_k, s_cmp_v, s_slc_k, s_slc_v, s_mla_ckv, s_mla_krope, s_moba_k, s_moba_v,
     s_win_k, s_win_v) = [jnp.stack(r, axis=0) for r in zip(*rows_s)]
    return (y_prompt, y_sample,
            p_cmp_k, p_cmp_v, p_slc_k, p_slc_v, p_mla_ckv, p_mla_krope, p_moba_k, p_moba_v,
            p_win_k, p_win_v, p_mem_k, p_mem_v,
            s_cmp_k, s_cmp_v, s_slc_k, s_slc_v, s_mla_ckv, s_mla_krope, s_moba_k, s_moba_v,
            s_win_k, s_win_v)
```

```python
import functools

import jax
import jax.numpy as jnp
from jax import lax
from jax.experimental import pallas as pl
from jax.experimental.pallas import tpu as pltpu

D_MODEL = 2048
BATCH = 2
SEQ = 4096
DEPTH = 4
DEC_BATCH = 128
DEC_SEQ = 1
PAST_LEN = 8192
PAGE_SIZE = 128

H_A = 8
HD_A = 64
CMP_LEN = 32
CMP_STRIDE = 16
CMP_HID = 256
SEL_BLOCK = 64
SEL_TOPN = 16
WINDOW = 512
H_B = 8
R_Q = 384
R_KV = 128
D_NOPE = 64
D_ROPE = 32
DV_B = 64
ROPE_THETA = 10000.0
H_C = 8
HD_C = 64
MOBA_BLOCK = 256
MOBA_TOPK = 3
N_MEM = 256
H_D = 4
HD_D = 128
N_BRANCH = 4
BR_WIDTH = 512
PEER_HEADS = 8
PEER_DK = 256
N_KEYS = 128
N_EXP = N_KEYS * N_KEYS
PEER_TOPK = 16
ALPHA = (2 * DEPTH) ** 0.25
BETA = (8 * DEPTH) ** -0.25
Q_BLOCK = 128
MOBA_Q_BLOCK = 32
TOK_BLOCK = 128
LN_EPS = 1e-5
RMS_EPS = 1e-6
NEG = -1e30
FORCE_SCORE = 1e9
IN_SIZES = (H_A * HD_A, 6 * HD_A, 3 * H_A, R_Q, R_KV, D_ROPE, H_C * HD_C, 2 * HD_C, H_D * HD_D, N_BRANCH * D_MODEL)
N_IN = sum(IN_SIZES)

VMEM_LIMIT_BYTES = 56 * 1024 * 1024


def _mm_kernel(x_ref, w_ref, o_ref):
    o_ref[...] = jnp.dot(x_ref[...].astype(jnp.bfloat16), w_ref[...].astype(jnp.bfloat16),
                         preferred_element_type=jnp.float32)


def _mm(x, w, tm=512, tn=512):
    M, K = x.shape
    K2, N = w.shape
    assert K == K2
    tm = min(tm, M)
    assert M % tm == 0
    tn = min(tn, N)
    grid = (pl.cdiv(N, tn), M // tm)
    return pl.pallas_call(
        _mm_kernel,
        grid=grid,
        in_specs=[pl.BlockSpec((tm, K), lambda j, i: (i, 0)),
                  pl.BlockSpec((K, tn), lambda j, i: (0, j))],
        out_specs=pl.BlockSpec((tm, tn), lambda j, i: (i, j)),
        out_shape=jax.ShapeDtypeStruct((M, N), jnp.float32),
        compiler_params=pltpu.CompilerParams(dimension_semantics=("arbitrary", "arbitrary"),
                                             vmem_limit_bytes=VMEM_LIMIT_BYTES),
        name="dense_mm",
    )(x, w)


def _mm3(x, w):
    B, T, K = x.shape
    return _mm(x.reshape(B * T, K), w).reshape(B, T, w.shape[1])


_NT = (((1,), (1,)), ((), ()))


def _moba_kernel(slope_ref, q_ref, k_ref, v_ref, o_ref, sel_ref, m_ref, l_ref, acc_ref, *, tq):
    i = pl.program_id(1)
    H = q_ref.shape[1]
    R = H * tq
    T = k_ref.shape[1]
    nb = T // MOBA_BLOCK
    scale = HD_C ** -0.5
    q = q_ref[0].reshape(R, HD_C).astype(jnp.bfloat16)
    row = lax.broadcasted_iota(jnp.int32, (R, 1), 0)
    t = i * tq + row % tq
    slope = slope_ref[...]
    cur = (i * tq) // MOBA_BLOCK
    km = jnp.concatenate(
        [jnp.sum(k_ref[0, pl.ds(j * MOBA_BLOCK, MOBA_BLOCK), :], axis=0, keepdims=True) for j in range(nb)],
        axis=0) * (1.0 / MOBA_BLOCK)
    gs = lax.dot_general(q, km.astype(jnp.bfloat16), _NT, preferred_element_type=jnp.float32)
    lane = lax.broadcasted_iota(jnp.int32, (R, nb), 1)
    past = lane < cur
    gs = jnp.where(past, gs, NEG)
    sel = jnp.zeros((R, nb), jnp.float32)
    for j in range(nb):
        col = gs[:, j:j + 1]
        beats = (gs > col) | ((gs == col) & (lane < j))
        rank = jnp.sum(beats.astype(jnp.float32), axis=1, keepdims=True)
        sel = jnp.where((lane == j) & (rank < MOBA_TOPK), 1.0, sel)
    sel_ref[...] = jnp.where(past, sel, 0.0)
    m_ref[...] = jnp.full((R, 1), NEG, jnp.float32)
    l_ref[...] = jnp.zeros((R, 1), jnp.float32)
    acc_ref[...] = jnp.zeros((R, HD_C), jnp.float32)
    kiota = lax.broadcasted_iota(jnp.int32, (1, MOBA_BLOCK), 1)

    def block(j, ok):
        start = pl.multiple_of(j * MOBA_BLOCK, MOBA_BLOCK)
        kb = k_ref[0, pl.ds(start, MOBA_BLOCK), :].astype(jnp.bfloat16)
        vb = v_ref[0, pl.ds(start, MOBA_BLOCK), :].astype(jnp.bfloat16)
        kp = j * MOBA_BLOCK + kiota
        s = lax.dot_general(q, kb, _NT, preferred_element_type=jnp.float32) * scale
        s = s - slope * (t - kp).astype(jnp.float32)
        s = jnp.where(ok, s, NEG)
        m_old = m_ref[...]
        m_new = jnp.maximum(m_old, jnp.max(s, axis=1, keepdims=True))
        p = jnp.where(ok, jnp.exp(s - m_new), 0.0)
        alpha = jnp.exp(m_old - m_new)
        l_ref[...] = alpha * l_ref[...] + jnp.sum(p, axis=1, keepdims=True)
        acc_ref[...] = alpha * acc_ref[...] + jnp.dot(p.astype(jnp.bfloat16), vb, preferred_element_type=jnp.float32)
        m_ref[...] = m_new

    def past_block(j, carry):
        sel_j = jnp.sum(jnp.where(lane == j, sel_ref[...], 0.0), axis=1, keepdims=True)
        block(j, sel_j > 0.5)
        return carry

    lax.fori_loop(0, cur, past_block, 0)
    block(cur, (cur * MOBA_BLOCK + kiota) <= t)
    o = acc_ref[...] / jnp.maximum(l_ref[...], 1e-30)
    o_ref[0] = o.reshape(H, tq, HD_C)


def _moba_prompt(q, k, v, tq=128):
    B, T, _ = k.shape
    assert T % MOBA_BLOCK == 0 and MOBA_BLOCK % tq == 0
    R = H_C * tq
    qh = q.reshape(B, T, H_C, HD_C).transpose(0, 2, 1, 3)
    slope = jnp.repeat(alibi_slopes(H_C), tq)[:, None]
    nb = T // MOBA_BLOCK
    o = pl.pallas_call(
        functools.partial(_moba_kernel, tq=tq),
        grid=(B, T // tq),
        in_specs=[pl.BlockSpec((R, 1), lambda b, i: (0, 0)),
                  pl.BlockSpec((1, H_C, tq, HD_C), lambda b, i: (b, 0, i, 0)),
                  pl.BlockSpec((1, T, HD_C), lambda b, i: (b, 0, 0)),
                  pl.BlockSpec((1, T, HD_C), lambda b, i: (b, 0, 0))],
        out_specs=pl.BlockSpec((1, H_C, tq, HD_C), lambda b, i: (b, 0, i, 0)),
        out_shape=jax.ShapeDtypeStruct((B, H_C, T, HD_C), jnp.float32),
        scratch_shapes=[pltpu.VMEM((R, nb), jnp.float32), pltpu.VMEM((R, 1), jnp.float32),
                        pltpu.VMEM((R, 1), jnp.float32), pltpu.VMEM((R, HD_C), jnp.float32)],
        compiler_params=pltpu.CompilerParams(dimension_semantics=("arbitrary", "arbitrary"),
                                             vmem_limit_bytes=VMEM_LIMIT_BYTES),
        name="moba_prompt",
    )(slope, qh, k, v)
    return o.transpose(0, 2, 1, 3).reshape(B, T, H_C * HD_C)


def _softmax_step(s, ok, vb, m_ref, l_ref, acc_ref):
    s = jnp.where(ok, s, NEG)
    m_old = m_ref[...]
    m_new = jnp.maximum(m_old, jnp.max(s, axis=1, keepdims=True))
    p = jnp.where(ok, jnp.exp(s - m_new), 0.0)
    alpha = jnp.exp(m_old - m_new)
    l_ref[...] = alpha * l_ref[...] + jnp.sum(p, axis=1, keepdims=True)
    acc_ref[...] = alpha * acc_ref[...] + jnp.dot(p.astype(jnp.bfloat16), vb, preferred_element_type=jnp.float32)
    m_ref[...] = m_new


def _softmax_init(m_ref, l_ref, acc_ref):
    m_ref[...] = jnp.full(m_ref.shape, NEG, jnp.float32)
    l_ref[...] = jnp.zeros(l_ref.shape, jnp.float32)
    acc_ref[...] = jnp.zeros(acc_ref.shape, jnp.float32)


def _mla_kernel(qn_ref, qr_ref, ckv_ref, kr_ref, wuk_ref, wuv_ref, o_ref, m_ref, l_ref, acc_ref, *, tq, bk):
    i = pl.program_id(1)
    H = qn_ref.shape[1]
    R = H * tq
    scale = (D_NOPE + D_ROPE) ** -0.5
    ql = jnp.concatenate(
        [jnp.dot(qn_ref[0, h].astype(jnp.bfloat16), wuk_ref[h].astype(jnp.bfloat16), preferred_element_type=jnp.float32)
         for h in range(H)], axis=0).astype(jnp.bfloat16)
    qr = qr_ref[0].reshape(R, D_ROPE).astype(jnp.bfloat16)
    row = lax.broadcasted_iota(jnp.int32, (R, 1), 0)
    t = i * tq + row % tq
    kiota = lax.broadcasted_iota(jnp.int32, (1, bk), 1)
    _softmax_init(m_ref, l_ref, acc_ref)

    def block(j, carry):
        start = pl.multiple_of(j * bk, bk)
        cb = ckv_ref[0, pl.ds(start, bk), :].astype(jnp.bfloat16)
        kb = kr_ref[0, pl.ds(start, bk), :].astype(jnp.bfloat16)
        s = (lax.dot_general(ql, cb, _NT, preferred_element_type=jnp.float32)
             + lax.dot_general(qr, kb, _NT, preferred_element_type=jnp.float32)) * scale
        _softmax_step(s, (j * bk + kiota) <= t, cb, m_ref, l_ref, acc_ref)
        return carry

    lax.fori_loop(0, (i * tq + tq - 1) // bk + 1, block, 0)
    o_lat = (acc_ref[...] / jnp.maximum(l_ref[...], 1e-30)).astype(jnp.bfloat16)
    for h in range(H):
        o_ref[0, h] = jnp.dot(o_lat[h * tq:(h + 1) * tq], wuv_ref[h].astype(jnp.bfloat16),
                              preferred_element_type=jnp.float32)


def _mla_prompt(q_nope, q_rope, ckv, krope, w_uk, w_uv, tq=128, bk=512):
    B, T, H, _ = q_nope.shape
    assert T % tq == 0 and T % bk == 0
    R = H * tq
    qn = q_nope.transpose(0, 2, 1, 3)
    qr = q_rope.transpose(0, 2, 1, 3)
    wuk = w_uk.transpose(1, 2, 0)
    wuv = w_uv.transpose(1, 0, 2)
    o = pl.pallas_call(
        functools.partial(_mla_kernel, tq=tq, bk=bk),
        grid=(B, T // tq),
        in_specs=[pl.BlockSpec((1, H, tq, D_NOPE), lambda b, i: (b, 0, i, 0)),
                  pl.BlockSpec((1, H, tq, D_ROPE), lambda b, i: (b, 0, i, 0)),
                  pl.BlockSpec((1, T, R_KV), lambda b, i: (b, 0, 0)),
                  pl.BlockSpec((1, T, D_ROPE), lambda b, i: (b, 0, 0)),
                  pl.BlockSpec((H, D_NOPE, R_KV), lambda b, i: (0, 0, 0)),
                  pl.BlockSpec((H, R_KV, DV_B), lambda b, i: (0, 0, 0))],
        out_specs=pl.BlockSpec((1, H, tq, DV_B), lambda b, i: (b, 0, i, 0)),
        out_shape=jax.ShapeDtypeStruct((B, H, T, DV_B), jnp.float32),
        scratch_shapes=[pltpu.VMEM((R, 1), jnp.float32), pltpu.VMEM((R, 1), jnp.float32),
                        pltpu.VMEM((R, R_KV), jnp.float32)],
        compiler_params=pltpu.CompilerParams(dimension_semantics=("arbitrary", "arbitrary"),
                                             vmem_limit_bytes=VMEM_LIMIT_BYTES),
        name="mla_prompt",
    )(qn, qr, ckv, krope, wuk, wuv)
    return o.transpose(0, 2, 1, 3).reshape(B, T, H * DV_B)


_NSA_BK = 256


def _nsa_kernel(slope_ref, q_ref, g_ref, kc_ref, vc_ref, ov_ref, ks_ref, vs_ref, kw_ref, vw_ref, o_ref,
                m_ref, l_ref, acc_ref, *, tq):
    i = pl.program_id(1)
    H = q_ref.shape[1]
    R = H * tq
    NC = kc_ref.shape[1]
    n_sel = ov_ref.shape[1]
    scale = HD_A ** -0.5
    q = q_ref[0].reshape(R, HD_A).astype(jnp.bfloat16)
    row = lax.broadcasted_iota(jnp.int32, (R, 1), 0)
    t = i * tq + row % tq
    slope = slope_ref[...]

    cmp_end = CMP_STRIDE * lax.broadcasted_iota(jnp.int32, (1, NC), 1) + (CMP_LEN - 1)
    okc = cmp_end <= t
    s = lax.dot_general(q, kc_ref[0].astype(jnp.bfloat16), _NT, preferred_element_type=jnp.float32) * scale
    s = jnp.where(okc, s - slope * (t - cmp_end).astype(jnp.float32), NEG)
    p = jnp.where(okc, jnp.exp(s - jnp.max(s, axis=1, keepdims=True)), 0.0)
    p = (p / jnp.maximum(jnp.sum(p, axis=1, keepdims=True), 1e-30)).astype(jnp.bfloat16)
    o_cmp = jnp.dot(p, vc_ref[0].astype(jnp.bfloat16), preferred_element_type=jnp.float32)
    imp_rows = jnp.dot(p, ov_ref[...].astype(jnp.bfloat16), preferred_element_type=jnp.float32)
    imp = imp_rows[0:tq]
    for h in range(1, H):
        imp = imp + imp_rows[h * tq:(h + 1) * tq]

    tt = i * tq + lax.broadcasted_iota(jnp.int32, (tq, 1), 0)
    cur = tt // SEL_BLOCK
    blk = lax.broadcasted_iota(jnp.int32, (tq, n_sel), 1)
    valid = blk * SEL_BLOCK <= tt
    forced = (blk == 0) | (blk == cur) | (blk == cur - 1)
    imp = jnp.where(valid, jnp.where(forced, FORCE_SCORE, imp), NEG)
    sel = jnp.zeros((tq, n_sel), jnp.float32)
    for j in range(n_sel):
        col = imp[:, j:j + 1]
        beats = (imp > col) | ((imp == col) & (blk < j))
        rank = jnp.sum(beats.astype(jnp.float32), axis=1, keepdims=True)
        sel = jnp.where((blk == j) & (rank < SEL_TOPN), 1.0, sel)
    sel = jnp.where(valid, sel, 0.0).astype(jnp.bfloat16)

    bk = _NSA_BK
    kiota = lax.broadcasted_iota(jnp.int32, (1, bk), 1)
    e_row = lax.broadcasted_iota(jnp.int32, (n_sel, bk), 0)
    e_col = jnp.right_shift(lax.broadcasted_iota(jnp.int32, (n_sel, bk), 1), SEL_BLOCK.bit_length() - 1)
    _softmax_init(m_ref, l_ref, acc_ref)

    def slc_block(j, carry):
        start = pl.multiple_of(j * bk, bk)
        kb = ks_ref[0, pl.ds(start, bk), :].astype(jnp.bfloat16)
        vb = vs_ref[0, pl.ds(start, bk), :].astype(jnp.bfloat16)
        kp = j * bk + kiota
        expand = (e_row == j * (bk // SEL_BLOCK) + e_col).astype(jnp.bfloat16)
        picked = jnp.dot(sel, expand, preferred_element_type=jnp.float32)
        picked = jnp.concatenate([picked] * H, axis=0)
        s = lax.dot_general(q, kb, _NT, preferred_element_type=jnp.float32) * scale
        s = s - slope * (t - kp).astype(jnp.float32)
        _softmax_step(s, (picked > 0.5) & (kp <= t), vb, m_ref, l_ref, acc_ref)
        return carry

    lax.fori_loop(0, (i * tq + tq - 1) // bk + 1, slc_block, 0)
    o_slc = acc_ref[...] / jnp.maximum(l_ref[...], 1e-30)

    wiota = lax.broadcasted_iota(jnp.int32, (1, tq), 1)
    _softmax_init(m_ref, l_ref, acc_ref)

    def win_block(j, carry):
        start = pl.multiple_of(j * tq, tq)
        kb = kw_ref[0, pl.ds(start, tq), :].astype(jnp.bfloat16)
        vb = vw_ref[0, pl.ds(start, tq), :].astype(jnp.bfloat16)
        dw = t - (j * tq + wiota)
        s = lax.dot_general(q, kb, _NT, preferred_element_type=jnp.float32) * scale
        s = s - slope * dw.astype(jnp.float32)
        _softmax_step(s, (dw >= 0) & (dw < WINDOW), vb, m_ref, l_ref, acc_ref)
        return carry

    lax.fori_loop(jnp.maximum(i - WINDOW // tq, 0), i + 1, win_block, 0)
    o_win = acc_ref[...] / jnp.maximum(l_ref[...], 1e-30)

    g = jax.nn.sigmoid(g_ref[0])
    o = (g[0].reshape(R, 1) * o_cmp + g[1].reshape(R, 1) * o_slc + g[2].reshape(R, 1) * o_win)
    o_ref[0] = o.reshape(H, tq, HD_A)


def _nsa_prompt(q, g_raw, kc, vc, k_slc, v_slc, k_win, v_win, tq=128):
    B, T, _ = k_slc.shape
    n_cmp = kc.shape[1]
    assert T % _NSA_BK == 0 and _NSA_BK % tq == 0 and WINDOW % tq == 0 and T % SEL_BLOCK == 0
    R = H_A * tq
    NC = -(-n_cmp // 128) * 128
    n_sel = T // SEL_BLOCK
    kc = jnp.pad(kc, ((0, 0), (0, NC - n_cmp), (0, 0)))
    vc = jnp.pad(vc, ((0, 0), (0, NC - n_cmp), (0, 0)))
    c0 = CMP_STRIDE * jnp.arange(NC)
    s0 = SEL_BLOCK * jnp.arange(n_sel)
    overlap = jnp.clip(jnp.minimum(c0[:, None] + CMP_LEN, s0[None, :] + SEL_BLOCK)
                       - jnp.maximum(c0[:, None], s0[None, :]), 0).astype(jnp.float32) / CMP_LEN
    overlap = jnp.where((jnp.arange(NC) < n_cmp)[:, None], overlap, 0.0)
    qh = q.reshape(B, T, H_A, HD_A).transpose(0, 2, 1, 3)
    gh = g_raw.reshape(B, T, H_A, 3).transpose(0, 3, 2, 1)[..., None]
    slope = jnp.repeat(alibi_slopes(H_A), tq)[:, None]
    row_spec = pl.BlockSpec((1, T, HD_A), lambda b, i: (b, 0, 0))
    o = pl.pallas_call(
        functools.partial(_nsa_kernel, tq=tq),
        grid=(B, T // tq),
        in_specs=[pl.BlockSpec((R, 1), lambda b, i: (0, 0)),
                  pl.BlockSpec((1, H_A, tq, HD_A), lambda b, i: (b, 0, i, 0)),
                  pl.BlockSpec((1, 3, H_A, tq, 1), lambda b, i: (b, 0, 0, i, 0)),
                  pl.BlockSpec((1, NC, HD_A), lambda b, i: (b, 0, 0)),
                  pl.BlockSpec((1, NC, HD_A), lambda b, i: (b, 0, 0)),
                  pl.BlockSpec((NC, n_sel), lambda b, i: (0, 0)),
                  row_spec, row_spec, row_spec, row_spec],
        out_specs=pl.BlockSpec((1, H_A, tq, HD_A), lambda b, i: (b, 0, i, 0)),
        out_shape=jax.ShapeDtypeStruct((B, H_A, T, HD_A), jnp.float32),
        scratch_shapes=[pltpu.VMEM((R, 1), jnp.float32), pltpu.VMEM((R, 1), jnp.float32),
                        pltpu.VMEM((R, HD_A), jnp.float32)],
        compiler_params=pltpu.CompilerParams(dimension_semantics=("arbitrary", "arbitrary"),
                                             vmem_limit_bytes=VMEM_LIMIT_BYTES),
        name="nsa_prompt",
    )(slope, qh, gh, kc, vc, overlap, k_slc, v_slc, k_win, v_win)
    return o.transpose(0, 2, 1, 3).reshape(B, T, H_A * HD_A)


_BIG_NEG = -3.0e38
_PEER_HALF = PEER_DK // 2


def _gelu_exact(x):
    return 0.5 * x * (1.0 + lax.erf(x * (2.0 ** -0.5)))


def _top_vals(x, k):
    n, T = x.shape
    riota = lax.broadcasted_iota(jnp.int32, (n, T), 0)
    kiota = lax.broadcasted_iota(jnp.int32, (k, T), 0)

    def body(r, carry):
        x, out = carry
        m = jnp.max(x, axis=0, keepdims=True)
        first = jnp.min(jnp.where(x == m, riota, n), axis=0, keepdims=True)
        out = jnp.where(kiota == r, m, out)
        x = jnp.where(riota == first, _BIG_NEG, x)
        return x, out

    _, out = lax.fori_loop(0, k, body, (x, jnp.zeros((k, T), jnp.float32)))
    return out


def _peer_router_kernel(x_ref, wq_ref, keys_ref, s1_ref, s2_ref, scal_ref, xb_ref):
    h = pl.program_id(1)
    xb = x_ref[...].astype(jnp.bfloat16)

    @pl.when(h == 0)
    def _():
        xb_ref[...] = xb

    q = jnp.dot(xb, wq_ref[...].astype(jnp.bfloat16), preferred_element_type=jnp.float32).astype(jnp.bfloat16)
    s1 = lax.dot_general(keys_ref[0].astype(jnp.bfloat16), q[:, :_PEER_HALF], _NT, preferred_element_type=jnp.float32)
    s2 = lax.dot_general(keys_ref[1].astype(jnp.bfloat16), q[:, _PEER_HALF:], _NT, preferred_element_type=jnp.float32)
    s1_ref[0] = s1
    s2_ref[0] = s2
    v1 = _top_vals(s1, PEER_TOPK)
    v2 = _top_vals(s2, PEER_TOPK)
    cand = jnp.concatenate([v1[0:1] + v2]
                           + [v1[a:a + 1] + v2[0:8] for a in range(1, 8)]
                           + [v1[8:16] + v2[0:1]], axis=0)
    sv = _top_vals(cand, PEER_TOPK)
    inv_z = 1.0 / jnp.sum(jnp.exp(sv - sv[0:1]), axis=0, keepdims=True)
    zero = jnp.zeros_like(inv_z)
    scal_ref[0] = jnp.concatenate([v1[0:1], v2[0:1], sv[PEER_TOPK - 1:PEER_TOPK], inv_z, zero, zero, zero, zero], axis=0)


def _peer_expert_kernel(xb_ref, s1_ref, s2_ref, scal_ref, u_ref, vt_ref, o_ref, e2_ref, *, ec):
    c = pl.program_id(1)

    @pl.when(c == 0)
    def _():
        o_ref[...] = jnp.zeros(o_ref.shape, jnp.float32)
        for h in range(PEER_HEADS):
            e2_ref[h] = jnp.exp(s2_ref[h] - scal_ref[h, 1:2, :]) * scal_ref[h, 3:4, :]

    ht = lax.dot_general(u_ref[...], xb_ref[...], _NT, preferred_element_type=jnp.float32)
    act = _gelu_exact(ht)
    pieces = []
    for kk in range(ec // N_KEYS):
        i1 = c * (ec // N_KEYS) + kk
        w = jnp.zeros((N_KEYS, ht.shape[1]), jnp.float32)
        for h in range(PEER_HEADS):
            s1row = s1_ref[h, pl.ds(i1, 1), :]
            e1row = jnp.exp(s1row - scal_ref[h, 0:1, :])
            w = w + jnp.where(s1row + s2_ref[h] >= scal_ref[h, 2:3, :], e1row * e2_ref[h], 0.0)
        pieces.append(w * act[kk * N_KEYS:(kk + 1) * N_KEYS])
    wt = jnp.concatenate(pieces, axis=0).astype(jnp.bfloat16)
    o_ref[...] += jnp.dot(vt_ref[...], wt, preferred_element_type=jnp.float32)


def _peer(x, wq, sub_keys, u_b16, vt_b16, tile=512, ec=512):
    N, D = x.shape
    T = min(tile, N)
    assert N % T == 0 and N_EXP % ec == 0 and ec % N_KEYS == 0
    nt = N // T
    cparams = pltpu.CompilerParams(dimension_semantics=("arbitrary", "arbitrary"), vmem_limit_bytes=VMEM_LIMIT_BYTES)
    s1, s2, scal, xb = pl.pallas_call(
        _peer_router_kernel,
        grid=(nt, PEER_HEADS),
        in_specs=[pl.BlockSpec((T, D), lambda i, h: (i, 0)),
                  pl.BlockSpec((D, PEER_DK), lambda i, h: (0, h)),
                  pl.BlockSpec((2, N_KEYS, _PEER_HALF), lambda i, h: (0, 0, 0))],
        out_specs=[pl.BlockSpec((1, N_KEYS, T), lambda i, h: (h, 0, i)),
                   pl.BlockSpec((1, N_KEYS, T), lambda i, h: (h, 0, i)),
                   pl.BlockSpec((1, 8, T), lambda i, h: (h, 0, i)),
                   pl.BlockSpec((T, D), lambda i, h: (i, 0))],
        out_shape=[jax.ShapeDtypeStruct((PEER_HEADS, N_KEYS, N), jnp.float32),
                   jax.ShapeDtypeStruct((PEER_HEADS, N_KEYS, N), jnp.float32),
                   jax.ShapeDtypeStruct((PEER_HEADS, 8, N), jnp.float32),
                   jax.ShapeDtypeStruct((N, D), jnp.bfloat16)],
        compiler_params=cparams,
        name="peer_router",
    )(x, wq, sub_keys)
    out_t = pl.pallas_call(
        functools.partial(_peer_expert_kernel, ec=ec),
        grid=(nt, N_EXP // ec),
        in_specs=[pl.BlockSpec((T, D), lambda i, c: (i, 0)),
                  pl.BlockSpec((PEER_HEADS, N_KEYS, T), lambda i, c: (0, 0, i)),
                  pl.BlockSpec((PEER_HEADS, N_KEYS, T), lambda i, c: (0, 0, i)),
                  pl.BlockSpec((PEER_HEADS, 8, T), lambda i, c: (0, 0, i)),
                  pl.BlockSpec((ec, D), lambda i, c: (c, 0)),
                  pl.BlockSpec((D, ec), lambda i, c: (0, c))],
        out_specs=pl.BlockSpec((D, T), lambda i, c: (0, i)),
        out_shape=jax.ShapeDtypeStruct((D, N), jnp.float32),
        scratch_shapes=[pltpu.VMEM((PEER_HEADS, N_KEYS, T), jnp.float32)],
        compiler_params=cparams,
        name="peer_experts",
    )(xb, s1, s2, scal, u_b16, vt_b16)
    return out_t.T


def layer_norm(x, g, b):
    xf = x.astype(jnp.float32)
    mu = jnp.mean(xf, -1, keepdims=True)
    var = jnp.mean(jnp.square(xf - mu), -1, keepdims=True)
    return ((xf - mu) * lax.rsqrt(var + LN_EPS) * g + b).astype(x.dtype)


def rms_norm(x, g):
    xf = x.astype(jnp.float32)
    return (xf * lax.rsqrt(jnp.mean(xf * xf, -1, keepdims=True) + RMS_EPS) * g).astype(x.dtype)


def rope(x, pos):
    half = D_ROPE // 2
    freqs = jnp.power(ROPE_THETA, -jnp.arange(half, dtype=jnp.float32) / half)
    ang = pos.astype(jnp.float32)[:, None] * freqs[None, :]
    ang = ang.reshape((1, pos.shape[0]) + (1,) * (x.ndim - 3) + (half,))
    cos, sin = jnp.cos(ang), jnp.sin(ang)
    x1 = x[..., :half].astype(jnp.float32)
    x2 = x[..., half:].astype(jnp.float32)
    return jnp.concatenate([x1 * cos - x2 * sin, x1 * sin + x2 * cos], -1).astype(x.dtype)


def alibi_slopes(n):
    return jnp.power(2.0, -8.0 * jnp.arange(1, n + 1, dtype=jnp.float32) / n)


def masked_softmax(s, mask):
    s = jnp.where(mask, s, NEG)
    m = jnp.max(s, -1, keepdims=True)
    p = jnp.exp(s - m) * mask
    return p / jnp.maximum(jnp.sum(p, -1, keepdims=True), 1e-30)


def split_cols(y, sizes):
    offs, acc = [], 0
    for s in sizes[:-1]:
        acc += s
        offs.append(acc)
    return jnp.split(y, offs, axis=-1)


def sweep_queries(fn, q_pos, qs, block):
    n = q_pos.shape[0]
    qb = block if n % block == 0 else n
    nc = n // qb
    pos_c = q_pos.reshape(nc, qb)
    qs_c = tuple(jnp.moveaxis(q.reshape((q.shape[0], nc, qb) + q.shape[2:]), 1, 0) for q in qs)
    out = lax.map(lambda a: fn(a[0], *a[1:]), (pos_c,) + qs_c)
    out = jnp.moveaxis(out, 0, 1)
    return out.reshape((out.shape[0], n) + out.shape[3:])


def gather_pages(pool, page_table):
    g = pool[page_table]
    return g.reshape((page_table.shape[0], page_table.shape[1] * pool.shape[1]) + pool.shape[2:])


def compress_rows(rows, cmp_pos, w1, b1, w2):
    B, L, _ = rows.shape
    n_cmp = (L - CMP_LEN) // CMP_STRIDE + 1
    cmp_idx = CMP_STRIDE * jnp.arange(n_cmp)[:, None] + jnp.arange(CMP_LEN)[None, :]
    blk = (rows[:, cmp_idx] + cmp_pos).reshape(B * n_cmp, CMP_LEN * HD_A)
    pad = -(B * n_cmp) % 8
    blk = jnp.pad(blk, ((0, pad), (0, 0)))
    hid = jax.nn.gelu(_mm(blk, w1) + b1, approximate=False)
    return _mm(hid, w2)[:B * n_cmp].reshape(B, n_cmp, HD_A)


def nsa_attention(q, gate, q_pos, k_cmp, v_cmp, k_slc, v_slc, k_win, v_win, win_start,
                  cmp_pos, cmp_w1, cmp_b1, cmp_w2):
    B, L, _ = k_cmp.shape
    scale = HD_A ** -0.5
    slopes = alibi_slopes(H_A)
    n_cmp = (L - CMP_LEN) // CMP_STRIDE + 1
    cmp_idx = CMP_STRIDE * jnp.arange(n_cmp)[:, None] + jnp.arange(CMP_LEN)[None, :]

    def compress(rows, i):
        blk = rows[:, cmp_idx] + cmp_pos[i]
        hid = jax.nn.gelu(blk.reshape(B, n_cmp, CMP_LEN * HD_A) @ cmp_w1[i] + cmp_b1[i], approximate=False)
        return hid @ cmp_w2[i]

    kc, vc = compress(k_cmp, 0), compress(v_cmp, 1)
    cmp_end = cmp_idx[:, -1]
    n_sel = -(-L // SEL_BLOCK)
    pad = n_sel * SEL_BLOCK - L
    ks_b = jnp.pad(k_slc, ((0, 0), (0, pad), (0, 0))).reshape(B, n_sel, SEL_BLOCK, HD_A)
    vs_b = jnp.pad(v_slc, ((0, 0), (0, pad), (0, 0))).reshape(B, n_sel, SEL_BLOCK, HD_A)
    c0 = CMP_STRIDE * jnp.arange(n_cmp)
    s0 = SEL_BLOCK * jnp.arange(n_sel)
    overlap = jnp.clip(jnp.minimum(c0[:, None] + CMP_LEN, s0[None, :] + SEL_BLOCK)
                       - jnp.maximum(c0[:, None], s0[None, :]), 0).astype(jnp.float32) / CMP_LEN
    k_top = min(SEL_TOPN, n_sel)
    blk_ids = jnp.arange(n_sel)
    kw_pad = jnp.pad(k_win, ((0, 0), (WINDOW, 0), (0, 0)))
    vw_pad = jnp.pad(v_win, ((0, 0), (WINDOW, 0), (0, 0)))

    def chunk(tp, qc, gc):
        qb = tp.shape[0]
        dist = (tp[:, None] - cmp_end[None, :]).astype(jnp.float32)
        s = jnp.einsum('bqhd,bcd->bqhc', qc, kc).astype(jnp.float32) * scale - slopes[:, None] * dist[:, None, :]
        p_cmp = masked_softmax(s, (cmp_end[None, :] <= tp[:, None])[:, None, :])
        o_cmp = jnp.einsum('bqhc,bcd->bqhd', p_cmp.astype(vc.dtype), vc)
        imp = jnp.einsum('bqhc,cj->bqj', p_cmp, overlap)
        cur = tp // SEL_BLOCK
        valid = blk_ids[None, :] * SEL_BLOCK <= tp[:, None]
        forced = (blk_ids[None, :] == 0) | (blk_ids[None, :] == cur[:, None]) | (blk_ids[None, :] == cur[:, None] - 1)
        imp = jnp.where(valid, jnp.where(forced, FORCE_SCORE, imp), NEG)
        _, sel = lax.top_k(imp, k_top)
        kg = jax.vmap(lambda kb, s_: kb[s_])(ks_b, sel).reshape(B, qb, k_top * SEL_BLOCK, HD_A)
        vg = jax.vmap(lambda vb, s_: vb[s_])(vs_b, sel).reshape(B, qb, k_top * SEL_BLOCK, HD_A)
        kp = (sel[..., None] * SEL_BLOCK + jnp.arange(SEL_BLOCK)).reshape(B, qb, k_top * SEL_BLOCK)
        ok = kp <= tp[None, :, None]
        s = (jnp.einsum('bqhd,bqkd->bqhk', qc, kg).astype(jnp.float32) * scale
             - slopes[None, None, :, None] * (tp[None, :, None, None] - kp[:, :, None, :]).astype(jnp.float32))
        o_slc = jnp.einsum('bqhk,bqkd->bqhd', masked_softmax(s, ok[:, :, None, :]).astype(vg.dtype), vg)
        wb = WINDOW + qb - 1
        start = tp[0] + 1 - win_start
        kw = lax.dynamic_slice_in_dim(kw_pad, start, wb, axis=1)
        vw = lax.dynamic_slice_in_dim(vw_pad, start, wb, axis=1)
        bp = tp[0] - WINDOW + 1 + jnp.arange(wb)
        dw = tp[:, None] - bp[None, :]
        okw = (bp[None, :] >= win_start) & (dw >= 0) & (dw < WINDOW)
        s = jnp.einsum('bqhd,bkd->bqhk', qc, kw).astype(jnp.float32) * scale - slopes[:, None] * dw.astype(jnp.float32)[:, None, :]
        o_win = jnp.einsum('bqhk,bkd->bqhd', masked_softmax(s, okw[:, None, :]).astype(vw.dtype), vw)
        return gc[..., 0:1] * o_cmp + gc[..., 1:2] * o_slc + gc[..., 2:3] * o_win

    return sweep_queries(chunk, q_pos, (q, gate), Q_BLOCK)


def mla_attention(q_nope, q_rope, q_pos, ckv, krope, w_uk, w_uv):
    L = ckv.shape[1]
    kpos = jnp.arange(L)
    scale = (D_NOPE + D_ROPE) ** -0.5
    q_lat = jnp.einsum('bqhn,rhn->bqhr', q_nope, w_uk)

    def chunk(tp, ql, qr):
        s = (jnp.einsum('bqhr,bkr->bqhk', ql, ckv) + jnp.einsum('bqhe,bke->bqhk', qr, krope)).astype(jnp.float32) * scale
        p = masked_softmax(s, (kpos[None, :] <= tp[:, None])[:, None, :])
        return jnp.einsum('bqhk,bkr->bqhr', p.astype(ckv.dtype), ckv)

    o_lat = sweep_queries(chunk, q_pos, (q_lat, q_rope), Q_BLOCK)
    return jnp.einsum('bqhr,rhv->bqhv', o_lat, w_uv)


def moba_attention(q, q_pos, k, v):
    B, L, _ = k.shape
    scale = HD_C ** -0.5
    slopes = alibi_slopes(H_C)
    nb = -(-L // MOBA_BLOCK)
    pad = nb * MOBA_BLOCK - L
    kb = jnp.pad(k, ((0, 0), (0, pad), (0, 0))).reshape(B, nb, MOBA_BLOCK, HD_C)
    vb = jnp.pad(v, ((0, 0), (0, pad), (0, 0))).reshape(B, nb, MOBA_BLOCK, HD_C)
    kmean = jnp.mean(kb.astype(jnp.float32), axis=2).astype(k.dtype)
    k_top = min(MOBA_TOPK, nb)
    blk_ids = jnp.arange(nb)
    in_blk = jnp.arange(MOBA_BLOCK)
    n_sk = k_top * MOBA_BLOCK

    def chunk(tp, qc):
        qb = tp.shape[0]
        cur = tp // MOBA_BLOCK
        gs = jnp.einsum('bqhd,bnd->bqhn', qc, kmean).astype(jnp.float32)
        past = blk_ids[None, :] < cur[:, None]
        gs = jnp.where(past[:, None, :], gs, NEG)
        _, sel = lax.top_k(gs, k_top)
        sel_ok = sel < cur[None, :, None, None]
        kg = jax.vmap(lambda kb_, s_: kb_[s_])(kb, sel)
        vg = jax.vmap(lambda vb_, s_: vb_[s_])(vb, sel)
        kp = sel[..., None] * MOBA_BLOCK + in_blk
        s_sel = (jnp.einsum('bqhd,bqhkmd->bqhkm', qc, kg).astype(jnp.float32) * scale
                 - slopes[None, None, :, None, None] * (tp[None, :, None, None, None] - kp).astype(jnp.float32))
        ok_sel = jnp.broadcast_to(sel_ok[..., None], kp.shape)
        ko = kb[:, cur]
        vo = vb[:, cur]
        op = cur[:, None] * MOBA_BLOCK + in_blk[None, :]
        s_own = (jnp.einsum('bqhd,bqmd->bqhm', qc, ko).astype(jnp.float32) * scale
                 - slopes[:, None] * (tp[:, None] - op).astype(jnp.float32)[:, None, :])
        ok_own = jnp.broadcast_to((op <= tp[:, None])[None, :, None, :], s_own.shape)
        s_all = jnp.concatenate([s_sel.reshape(B, qb, H_C, n_sk), s_own], -1)
        ok_all = jnp.concatenate([ok_sel.reshape(B, qb, H_C, n_sk), ok_own], -1)
        p = masked_softmax(s_all, ok_all).astype(v.dtype)
        return (jnp.einsum('bqhkm,bqhkmd->bqhd', p[..., :n_sk].reshape(B, qb, H_C, k_top, MOBA_BLOCK), vg)
                + jnp.einsum('bqhm,bqmd->bqhd', p[..., n_sk:], vo))

    return sweep_queries(chunk, q_pos, (q,), MOBA_Q_BLOCK)


def memory_attention(q, mk, mv):
    s = jnp.einsum('bqhd,bmhd->bqhm', q, mk).astype(jnp.float32) * HD_D ** -0.5
    p = jax.nn.softmax(s, axis=-1).astype(mv.dtype)
    return jnp.einsum('bqhm,bmhd->bqhd', p, mv)


def trunk_layer(x, start, past, mem_k, mem_v, p):
    B, T, _ = x.shape
    pos = start + jnp.arange(T, dtype=jnp.int32)
    a_q, a_kv, a_g, b_cq, b_ckv, b_kr, c_q, c_kv, d_q, g_pre = split_cols(_mm3(x, p['w_in']), IN_SIZES)
    a_q = a_q.reshape(B, T, H_A, HD_A)
    a_g_raw = a_g
    a_g = jax.nn.sigmoid(a_g.reshape(B, T, H_A, 3))
    k_cmp, v_cmp, k_slc, v_slc, k_win, v_win = jnp.split(a_kv, 6, axis=-1)
    q_b = _mm3(rms_norm(b_cq, p['q_norm']), p['w_uq'].reshape(R_Q, H_B * (D_NOPE + D_ROPE)))
    q_b = q_b.reshape(B, T, H_B, D_NOPE + D_ROPE)
    q_nope, q_rope = q_b[..., :D_NOPE], rope(q_b[..., D_NOPE:], pos)
    ckv = rms_norm(b_ckv, p['kv_norm'])
    krope = rope(b_kr, pos)
    c_q = c_q.reshape(B, T, H_C, HD_C)
    k_mb, v_mb = jnp.split(c_kv, 2, axis=-1)
    d_q = d_q.reshape(B, T, H_D, HD_D)
    gates = jax.nn.sigmoid(g_pre.reshape(B, T, N_BRANCH, D_MODEL))
    rows = (k_cmp, v_cmp, k_slc, v_slc, ckv, krope, k_mb, v_mb)
    if past is None:
        full, win_k, win_v, buf_len = rows, k_win, v_win, min(WINDOW, T)
    else:
        full = tuple(jnp.concatenate([old, new], axis=1) for old, new in zip(past[:8], rows))
        win_k = jnp.concatenate([past[8], k_win], axis=1)
        win_v = jnp.concatenate([past[9], v_win], axis=1)
        buf_len = past[8].shape[1]
    win_start = start + T - win_k.shape[1]
    f_cmp_k, f_cmp_v, f_slc_k, f_slc_v, f_ckv, f_krope, f_mb_k, f_mb_v = full
    if past is None:
        kc = compress_rows(k_cmp, p['cmp_pos'][0], p['cmp_w1'][0], p['cmp_b1'][0], p['cmp_w2'][0])
        vc = compress_rows(v_cmp, p['cmp_pos'][1], p['cmp_w1'][1], p['cmp_b1'][1], p['cmp_w2'][1])
        o_a = _nsa_prompt(a_q.reshape(B, T, H_A * HD_A), a_g_raw, kc, vc, k_slc, v_slc, k_win, v_win)
        o_b = _mla_prompt(q_nope, q_rope, ckv, krope, p['w_uk'], p['w_uv'])
        o_c = _moba_prompt(c_q.reshape(B, T, H_C * HD_C), k_mb, v_mb)
    else:
        o_a = nsa_attention(a_q, a_g, pos, f_cmp_k, f_cmp_v, f_slc_k, f_slc_v, win_k, win_v, win_start,
                            p['cmp_pos'], p['cmp_w1'], p['cmp_b1'], p['cmp_w2'])
        o_b = mla_attention(q_nope, q_rope, pos, f_ckv, f_krope, p['w_uk'], p['w_uv'])
        o_c = moba_attention(c_q, pos, f_mb_k, f_mb_v)
    o_d = memory_attention(d_q, mem_k, mem_v)
    o = (o_a.reshape(B, T, BR_WIDTH), o_b.reshape(B, T, BR_WIDTH),
         o_c.reshape(B, T, BR_WIDTH), o_d.reshape(B, T, BR_WIDTH))
    mixed = 0.0
    for n in range(N_BRANCH):
        mixed = mixed + gates[:, :, n] * _mm3(o[n], p['w_branch'][n])
    mixed = _mm3(mixed, p['w_out'])
    h = layer_norm(ALPHA * x + mixed, p['ln1_g'], p['ln1_b'])
    f = _peer(h.reshape(B * T, D_MODEL), p['peer_wq'], p['peer_keys'], p['peer_u_b16'], p['peer_vt_b16']).reshape(B, T, D_MODEL)
    y = layer_norm(ALPHA * h + f, p['ln2_g'], p['ln2_b'])
    return y, rows + (win_k[:, -buf_len:], win_v[:, -buf_len:])


def kernel(x_prompt, x_sample, cache_cmp_k, cache_cmp_v, cache_slc_k, cache_slc_v, cache_mla_ckv,
           cache_mla_krope, cache_moba_k, cache_moba_v, state_win_k, state_win_v, cache_mem_k, cache_mem_v,
           page_table, mem_prompt, w_in, nsa_cmp_pos, nsa_cmp_w1, nsa_cmp_b1, nsa_cmp_w2, mla_q_norm,
           mla_kv_norm, mla_w_uq, mla_w_uk, mla_w_uv, w_mem_kv, w_branch, w_out, ln1_g, ln1_b, ln2_g, ln2_b,
           peer_wq, peer_keys, peer_u, peer_v):
    past_len = page_table.shape[1] * cache_cmp_k.shape[2]
    pools = (cache_cmp_k, cache_cmp_v, cache_slc_k, cache_slc_v, cache_mla_ckv, cache_mla_krope,
             cache_moba_k, cache_moba_v)
    y_prompt, y_sample = x_prompt, x_sample
    rows_p, rows_s = [], []
    for l in range(DEPTH):
        p = {
            'w_in': w_in[l], 'cmp_pos': nsa_cmp_pos[l], 'cmp_w1': nsa_cmp_w1[l], 'cmp_b1': nsa_cmp_b1[l],
            'cmp_w2': nsa_cmp_w2[l], 'q_norm': mla_q_norm[l], 'kv_norm': mla_kv_norm[l], 'w_uq': mla_w_uq[l],
            'w_uk': mla_w_uk[l], 'w_uv': mla_w_uv[l], 'w_branch': w_branch[l], 'w_out': w_out[l],
            'ln1_g': ln1_g[l], 'ln1_b': ln1_b[l], 'ln2_g': ln2_g[l], 'ln2_b': ln2_b[l],
            'peer_wq': peer_wq[l], 'peer_keys': peer_keys[l],
            'peer_u_b16': peer_u[l].astype(jnp.bfloat16), 'peer_vt_b16': peer_v[l].T.astype(jnp.bfloat16),
        }
        mem_kv = _mm3(mem_prompt, w_mem_kv[l]).reshape(mem_prompt.shape[0], N_MEM, 2, H_D, HD_D)
        mem_k, mem_v = mem_kv[:, :, 0], mem_kv[:, :, 1]
        y_prompt, new_p = trunk_layer(y_prompt, 0, None, mem_k, mem_v, p)
        past = tuple(gather_pages(c[l], page_table) for c in pools) + (state_win_k[l], state_win_v[l])
        y_sample, new_s = trunk_layer(y_sample, past_len, past, cache_mem_k[l], cache_mem_v[l], p)
        rows_p.append(new_p + (mem_k, mem_v))
        rows_s.append(new_s)
    outs_p = [jnp.stack(r, axis=0) for r in zip(*rows_p)]
    outs_s = [jnp.stack(r, axis=0) for r in zip(*rows_s)]
    return (y_prompt, y_sample) + tuple(outs_p) + tuple(outs_s)
```
